```python
import math
import jax, jax.numpy as jnp
from jax import lax
import numpy as np

D_MODEL = 1024
BATCH = 1
SEQ = 16384
DEPTH = 1

HEAD_DIM = 128
MIX_WIDTH = D_MODEL
GDN_WIDTH = MIX_WIDTH // 2
RET_WIDTH = MIX_WIDTH - GDN_WIDTH
GDN_HEADS = GDN_WIDTH // HEAD_DIM
RET_HEADS = RET_WIDTH // HEAD_DIM
SHORT_CONV = 4
FFN_CONV = 3
D_FF = ((8 * D_MODEL // 3 + 255) // 256) * 256
CHUNK = 64
RET_ROPE_BASE = 10000.0
EPS = 1e-6
IN_SPLITS = (GDN_WIDTH, GDN_WIDTH, GDN_WIDTH, GDN_WIDTH, GDN_HEADS, GDN_HEADS,
             RET_WIDTH, RET_WIDTH, RET_WIDTH, RET_WIDTH)
IN_COLS = 4 * GDN_WIDTH + 2 * GDN_HEADS + 4 * RET_WIDTH

kernel_name = "hybrid_gdn_retention_convffn"


def _rms_norm(x, w):
    xf = x.astype(jnp.float32)
    y = xf * lax.rsqrt(jnp.mean(xf * xf, axis=-1, keepdims=True) + EPS)
    return (y * w.astype(jnp.float32)).astype(x.dtype)


def _rms_f32(xf):
    return xf * lax.rsqrt(jnp.mean(xf * xf, axis=-1, keepdims=True) + EPS)


def _l2norm(xf):
    return xf * lax.rsqrt(jnp.sum(xf * xf, axis=-1, keepdims=True) + EPS)


def _causal_dwconv(x, w):
    k, c = w.shape
    return lax.conv_general_dilated(
        x, w[:, None, :].astype(x.dtype), window_strides=(1,), padding=[(k - 1, 0)],
        dimension_numbers=("NWC", "WIO", "NWC"), feature_group_count=c)


def _to_chunks(t):
    b, s, h, d = t.shape
    return t.reshape(b, s // CHUNK, CHUNK, h, d).transpose(1, 0, 3, 2, 4)


def _from_chunks(t):
    nc, b, h, c, d = t.shape
    return t.transpose(1, 0, 3, 2, 4).reshape(b, nc * c, h, d)


def _scalar_chunks(t):
    b, s, h = t.shape
    return t.reshape(b, s // CHUNK, CHUNK, h).transpose(1, 0, 3, 2)


def _gated_deltanet(q, k, v, gate, a, b, conv_w, a_log, dt_bias, norm_w):
    bsz, s, _ = q.shape
    h, d = GDN_HEADS, HEAD_DIM
    qkv = jax.nn.silu(_causal_dwconv(jnp.concatenate([q, k, v], axis=-1), conv_w))
    qkv = qkv.astype(jnp.float32)
    q, k, v = jnp.split(qkv, 3, axis=-1)
    q = _l2norm(q.reshape(bsz, s, h, d)) * (d ** -0.5)
    k = _l2norm(k.reshape(bsz, s, h, d))
    v = v.reshape(bsz, s, h, d)
    beta = jax.nn.sigmoid(b.astype(jnp.float32))
    g = -jnp.exp(a_log.astype(jnp.float32)) * jax.nn.softplus(
        a.astype(jnp.float32) + dt_bias.astype(jnp.float32))

    qc, kc, vc = _to_chunks(q), _to_chunks(k), _to_chunks(v)
    gcum = jnp.cumsum(_scalar_chunks(g), axis=-1)
    betac = _scalar_chunks(beta)
    causal = jnp.tril(jnp.ones((CHUNK, CHUNK), dtype=bool))
    strict = jnp.tril(jnp.ones((CHUNK, CHUNK), dtype=bool), -1)
    diff = gcum[..., :, None] - gcum[..., None, :]
    decay = jnp.exp(jnp.where(causal, diff, -jnp.inf))
    kb = kc * betac[..., None]
    vb = vc * betac[..., None]
    m = jnp.where(strict, jnp.einsum('nbhik,nbhjk->nbhij', kb, kc) * decay, 0.0)
    eye = jnp.eye(CHUNK, dtype=jnp.float32)
    t_mat = lax.linalg.triangular_solve(eye + m, jnp.broadcast_to(eye, m.shape),
                                        left_side=True, lower=True, unit_diagonal=True)
    u_base = jnp.einsum('nbhij,nbhjv->nbhiv', t_mat, vb)
    w_mat = jnp.einsum('nbhij,nbhjk->nbhik', t_mat, kb * jnp.exp(gcum)[..., None])
    qk = jnp.einsum('nbhik,nbhjk->nbhij', qc, kc) * decay
    q_dec = qc * jnp.exp(gcum)[..., None]
    k_dec = kc * jnp.exp(gcum[..., -1:] - gcum)[..., None]
    chunk_decay = jnp.exp(gcum[..., -1])

    def step(state, inp):
        u_b, w_c, qk_c, qd_c, kd_c, cd_c = inp
        u = u_b - jnp.einsum('bhck,bhkv->bhcv', w_c, state)
        o = jnp.einsum('bhck,bhkv->bhcv', qd_c, state) + jnp.einsum('bhij,bhjv->bhiv', qk_c, u)
        state = state * cd_c[..., None, None] + jnp.einsum('bhck,bhcv->bhkv', kd_c, u)
        return state, o

    s0 = jnp.zeros((bsz, h, d, d), dtype=jnp.float32)
    _, o = lax.scan(step, s0, (u_base, w_mat, qk, q_dec, k_dec, chunk_decay))
    o = _from_chunks(o)
    gt = gate.astype(jnp.float32).reshape(bsz, s, h, d)
    o = _rms_f32(o) * norm_w.astype(jnp.float32) * jax.nn.silu(gt)
    return o.reshape(bsz, s, h * d).astype(gate.dtype)


def _rotate_every_two(x):
    x1 = x[..., ::2]
    x2 = x[..., 1::2]
    return jnp.stack((-x2, x1), axis=-1).reshape(x.shape)


def _xpos_rotary(x, pos):
    d = x.shape[-1]
    angle = 1.0 / (RET_ROPE_BASE ** jnp.linspace(0.0, 1.0, d // 2, dtype=jnp.float32))
    angle = jnp.repeat(angle, 2)
    phase = pos[:, None] * angle[None, :]
    return x * jnp.cos(phase)[:, None, :] + _rotate_every_two(x) * jnp.sin(phase)[:, None, :]


def _retention(q, k, v, gate):
    bsz, s, _ = q.shape
    h, d = RET_HEADS, HEAD_DIM
    pos = jnp.arange(s, dtype=jnp.float32)
    q = _xpos_rotary(q.astype(jnp.float32).reshape(bsz, s, h, d), pos)
    k = _xpos_rotary(k.astype(jnp.float32).reshape(bsz, s, h, d), pos) * (d ** -0.5)
    v = v.astype(jnp.float32).reshape(bsz, s, h, d)
    log_gamma = jnp.log(1.0 - jnp.exp2(-5.0 - jnp.arange(h, dtype=jnp.float32)))
    idx = jnp.arange(CHUNK, dtype=jnp.float32)
    causal = jnp.tril(jnp.ones((CHUNK, CHUNK), dtype=bool))
    rel = idx[:, None] - idx[None, :]
    d_mat = jnp.exp(jnp.where(causal[None], rel[None] * log_gamma[:, None, None], -jnp.inf))
    xi = jnp.exp((idx[None, :] + 1.0) * log_gamma[:, None])[..., None]
    zeta = jnp.exp((CHUNK - 1.0 - idx[None, :]) * log_gamma[:, None])[..., None]
    chunk_gamma = jnp.exp(CHUNK * log_gamma)

    qc, kc, vc = _to_chunks(q), _to_chunks(k), _to_chunks(v)
    inner = jnp.einsum('nbhij,nbhjv->nbhiv',
                       jnp.einsum('nbhik,nbhjk->nbhij', qc, kc) * d_mat, vc)
    q_xi = qc * xi
    kv = jnp.einsum('nbhck,nbhcv->nbhkv', kc * zeta, vc)

    def step(r, inp):
        inner_c, qx_c, kv_c = inp
        o = inner_c + jnp.einsum('bhck,bhkv->bhcv', qx_c, r)
        r = r * chunk_gamma[:, None, None] + kv_c
        return r, o

    r0 = jnp.zeros((bsz, h, d, d), dtype=jnp.float32)
    _, o = lax.scan(step, r0, (inner, q_xi, kv))
    o = _rms_f32(_from_chunks(o))
    o = jax.nn.silu(gate.astype(jnp.float32).reshape(bsz, s, h, d)) * o
    return o.reshape(bsz, s, h * d).astype(gate.dtype)


def _conv_glu_mlp(x, w_up, conv_w, w_down):
    hdn = jnp.einsum('bsd,df->bsf', x, w_up)
    hdn = _causal_dwconv(hdn, conv_w)
    g, u = jnp.split(hdn, 2, axis=-1)
    return jnp.einsum('bsf,fd->bsd', jax.nn.silu(g) * u, w_down)


def setup_inputs(seed: int = 0) -> dict:
    key = jax.random.key(seed)
    ks = jax.random.split(key, 14)
    f32 = jnp.float32
    x = jax.random.normal(ks[0], (BATCH, SEQ, D_MODEL), f32)
    attn_norm_w = 1.0 + 0.02 * jax.random.normal(ks[1], (DEPTH, D_MODEL), f32)
    w_in = jax.random.normal(ks[2], (DEPTH, D_MODEL, IN_COLS), f32) * D_MODEL ** -0.5
    gdn_conv_w = jax.random.normal(ks[3], (DEPTH, SHORT_CONV, 3 * GDN_WIDTH), f32) * SHORT_CONV ** -0.5
    gdn_a_log = jnp.log(jax.random.uniform(ks[4], (DEPTH, GDN_HEADS), f32, minval=1.0, maxval=16.0))
    dt = jnp.exp(jax.random.uniform(ks[5], (DEPTH, GDN_HEADS), f32,
                                    minval=math.log(1e-3), maxval=math.log(1e-1)))
    gdn_dt_bias = dt + jnp.log(-jnp.expm1(-dt))
    gdn_norm_w = 1.0 + 0.02 * jax.random.normal(ks[6], (DEPTH, HEAD_DIM), f32)
    w_out = jax.random.normal(ks[7], (DEPTH, MIX_WIDTH, D_MODEL), f32) * MIX_WIDTH ** -0.5
    mlp_norm_w = 1.0 + 0.02 * jax.random.normal(ks[8], (DEPTH, D_MODEL), f32)
    w_up = jax.random.normal(ks[9], (DEPTH, D_MODEL, 2 * D_FF), f32) * D_MODEL ** -0.5
    mlp_conv_w = jax.random.normal(ks[10], (DEPTH, FFN_CONV, 2 * D_FF), f32) * FFN_CONV ** -0.5
    w_down = jax.random.normal(ks[11], (DEPTH, D_FF, D_MODEL), f32) * D_FF ** -0.5
    final_norm_w = 1.0 + 0.02 * jax.random.normal(ks[12], (D_MODEL,), f32)
    return {"x": x, "attn_norm_w": attn_norm_w, "w_in": w_in, "gdn_conv_w": gdn_conv_w,
            "gdn_a_log": gdn_a_log, "gdn_dt_bias": gdn_dt_bias, "gdn_norm_w": gdn_norm_w,
            "w_out": w_out, "mlp_norm_w": mlp_norm_w, "w_up": w_up, "mlp_conv_w": mlp_conv_w,
            "w_down": w_down, "final_norm_w": final_norm_w}


def reference(x, attn_norm_w, w_in, gdn_conv_w, gdn_a_log, gdn_dt_bias, gdn_norm_w,
              w_out, mlp_norm_w, w_up, mlp_conv_w, w_down, final_norm_w):
    split_idx = []
    acc = 0
    for sz in IN_SPLITS[:-1]:
        acc += sz
        split_idx.append(acc)
    h = x
    for l in range(DEPTH):
        n = _rms_norm(h, attn_norm_w[l])
        proj = jnp.einsum('bsd,dc->bsc', n, w_in[l])
        (gq, gk, gv, gg, ga, gb, rq, rk, rv, rg) = jnp.split(proj, split_idx, axis=-1)
        o_gdn = _gated_deltanet(gq, gk, gv, gg, ga, gb, gdn_conv_w[l], gdn_a_log[l],
                                gdn_dt_bias[l], gdn_norm_w[l])
        o_ret = _retention(rq, rk, rv, rg)
        mix = jnp.concatenate([o_gdn, o_ret], axis=-1)
        h = h + jnp.einsum('bsm,md->bsd', mix, w_out[l])
        n2 = _rms_norm(h, mlp_norm_w[l])
        h = h + _conv_glu_mlp(n2, w_up[l], mlp_conv_w[l], w_down[l])
    return _rms_norm(h, final_norm_w)
```

```python
import functools
import math

import numpy as np
import jax
import jax.numpy as jnp
from jax import lax
from jax.experimental import pallas as pl
from jax.experimental.pallas import tpu as pltpu

D_MODEL = 1024
HEAD_DIM = 128
N_HEADS = 4
GROUP_W = N_HEADS * HEAD_DIM
SHORT_CONV = 4
FFN_CONV = 3
D_FF = 2816
ROPE_BASE = 10000.0
EPS = 1e-6

TILE = 512
CHUNK = 128
N_CHUNKS = TILE // CHUNK
GDN_HALO = 8
MLP_HALO = 16
FF_BLOCK = 256
AB_PAD = 128
VMEM_LIMIT = 56 * 1024 * 1024

F32 = jnp.float32
BF16 = jnp.bfloat16


def _dot(a, b):
    return jnp.dot(a, b, preferred_element_type=F32)


def _dot_nt(a, b):
    return lax.dot_general(a, b, (((1,), (1,)), ((), ())), preferred_element_type=F32)


def _dot_tn(a, b):
    return lax.dot_general(a, b, (((0,), (0,)), ((), ())), preferred_element_type=F32)


def _silu(x):
    return x * jax.nn.sigmoid(x)


def _rms(x):
    return x * lax.rsqrt(jnp.mean(x * x, axis=-1, keepdims=True) + EPS)


def _inproj_kernel(x_ref, nw_ref, w_ref, gqkv_ref, gg_ref, rqkv_ref, rg_ref, ab_ref):
    n = (_rms(x_ref[...]) * nw_ref[...]).astype(BF16)
    col = 0
    for ref in (gqkv_ref, gg_ref, rqkv_ref, rg_ref, ab_ref):
        width = ref.shape[1]
        ref[...] = _dot(n, w_ref[:, col:col + width])
        col += width


def _inproj(x, norm_w, w_all):
    seq = x.shape[0]
    widths = (3 * GROUP_W, GROUP_W, 3 * GROUP_W, GROUP_W, AB_PAD)
    row = lambda i: (i, 0)
    fixed = lambda i: (0, 0)
    return pl.pallas_call(
        _inproj_kernel,
        grid=(seq // TILE,),
        in_specs=[
            pl.BlockSpec((TILE, D_MODEL), row),
            pl.BlockSpec((1, D_MODEL), fixed),
            pl.BlockSpec(w_all.shape, fixed, pipeline_mode=pl.Buffered(1)),
        ],
        out_specs=[pl.BlockSpec((TILE, w), row) for w in widths],
        out_shape=[jax.ShapeDtypeStruct((seq, w), F32) for w in widths],
        compiler_params=pltpu.CompilerParams(
            dimension_semantics=("arbitrary",), vmem_limit_bytes=VMEM_LIMIT),
        name="inproj",
    )(x, norm_w, w_all)


def _inv_unit_lower(m, diag_mask, eye):
    a = jnp.where(diag_mask, -m, 0.0)
    m_off = jnp.where(diag_mask, 0.0, m)
    p = eye + a
    for _ in range(4):
        ab = a.astype(BF16)
        a = _dot(ab, ab)
        p = p + _dot(p.astype(BF16), a.astype(BF16))
    d_inv = p.astype(BF16)
    n = _dot(d_inv, m_off.astype(BF16))
    nb = n.astype(BF16)
    n2 = _dot(nb, nb)
    outer = (eye - n) + n2 - _dot(nb, n2.astype(BF16))
    return _dot(outer.astype(BF16), d_inv)


def _mixer_kernel(x_ref, gqkv_ref, halo_ref, gg_ref, rqkv_ref, rg_ref, ab_ref,
                  convw_ref, alog_ref, dtb_ref, gnw_ref, ltri_ref,
                  dmat_ref, xi_ref, zeta_ref, cosr_ref, sinr_ref, cosb_ref, sinb_ref,
                  sgn_ref, wout_ref,
                  h_ref,
                  xbuf, q_s, k_s, v_s, gc_s, beta_s, cos_s, sin_s, mix_s, s_state, r_state,
                  *, chunk_gammas):
    step = pl.program_id(0)

    @pl.when(step == 0)
    def _():
        s_state[...] = jnp.zeros_like(s_state)
        r_state[...] = jnp.zeros_like(r_state)

    halo = halo_ref[...]
    xbuf[0:GDN_HALO, :] = jnp.where(step == 0, jnp.zeros_like(halo), halo)
    xbuf[GDN_HALO:, :] = gqkv_ref[...]
    for c in range(N_CHUNKS):
        r0 = c * CHUNK
        for g in range(3 * N_HEADS):
            lanes = slice(g * HEAD_DIM, (g + 1) * HEAD_DIM)
            base = r0 + GDN_HALO - (SHORT_CONV - 1)
            acc = convw_ref[0:1, lanes] * xbuf[base:base + CHUNK, lanes]
            for i in range(1, SHORT_CONV):
                acc = acc + convw_ref[i:i + 1, lanes] * xbuf[base + i:base + i + CHUNK, lanes]
            y = _silu(acc)
            out_lanes = slice((g % N_HEADS) * HEAD_DIM, (g % N_HEADS + 1) * HEAD_DIM)
            if g < N_HEADS:
                ss = jnp.sum(y * y, axis=-1, keepdims=True)
                q_s[r0:r0 + CHUNK, out_lanes] = y * (lax.rsqrt(ss + EPS) * HEAD_DIM ** -0.5)
            elif g < 2 * N_HEADS:
                ss = jnp.sum(y * y, axis=-1, keepdims=True)
                k_s[r0:r0 + CHUNK, out_lanes] = y * lax.rsqrt(ss + EPS)
            else:
                v_s[r0:r0 + CHUNK, out_lanes] = y

        ab = ab_ref[r0:r0 + CHUNK, :]
        z = ab + dtb_ref[...]
        softplus = jnp.maximum(z, 0.0) + jnp.log1p(jnp.exp(-jnp.abs(z)))
        g_step = -jnp.exp(alog_ref[...]) * softplus
        gc_s[r0:r0 + CHUNK, :] = jnp.dot(ltri_ref[...], g_step, preferred_element_type=F32,
                                         precision=lax.Precision.HIGHEST)
        beta_s[r0:r0 + CHUNK, :] = jax.nn.sigmoid(ab)

    cb = cosb_ref[0]
    sb = sinb_ref[0]
    cos_t = cosr_ref[...] * cb - sinr_ref[...] * sb
    sin_t = sinr_ref[...] * cb + cosr_ref[...] * sb
    cos_s[...] = cos_t
    sin_s[...] = sin_t * sgn_ref[...]

    row_i = lax.broadcasted_iota(jnp.int32, (CHUNK, CHUNK), 0)
    col_j = lax.broadcasted_iota(jnp.int32, (CHUNK, CHUNK), 1)
    causal = row_i >= col_j
    strict = row_i > col_j
    diag_mask = (row_i // 32) == (col_j // 32)
    eye = jnp.where(row_i == col_j, 1.0, 0.0).astype(F32)

    def chunk_body(c, carry):
        r0 = pl.multiple_of(c * CHUNK, CHUNK)
        rows = pl.ds(r0, CHUNK)
        gc_all = gc_s[rows, :]
        gc_t = gc_all.T
        beta_all = beta_s[rows, :]

        for h in range(N_HEADS):
            lanes = slice(h * HEAD_DIM, (h + 1) * HEAD_DIM)
            q = q_s[rows, lanes]
            k = k_s[rows, lanes]
            v = v_s[rows, lanes]
            gcol = jnp.broadcast_to(gc_all[:, h:h + 1], (CHUNK, HEAD_DIM))
            grow = jnp.broadcast_to(gc_t[h:h + 1, :], (CHUNK, CHUNK))
            beta = jnp.broadcast_to(beta_all[:, N_HEADS + h:N_HEADS + h + 1], (CHUNK, HEAD_DIM))
            glast = gcol[CHUNK - 1:CHUNK, :]
            decay = jnp.exp(jnp.where(causal, gcol - grow, -1e30))
            eg = jnp.exp(gcol)
            kb = k.astype(BF16)
            gram = _dot_nt(jnp.concatenate([q, k], axis=0).astype(BF16), kb)
            qk = gram[:CHUNK] * decay
            m = jnp.where(strict, beta * gram[CHUNK:] * decay, 0.0)
            t_mat = _inv_unit_lower(m, diag_mask, eye)
            rhs = jnp.concatenate([v * beta, k * (beta * eg)], axis=1).astype(BF16)
            uw = _dot(t_mat.astype(BF16), rhs)
            state = s_state[h]
            ws = _dot(jnp.concatenate([uw[:, HEAD_DIM:], q * eg], axis=0).astype(BF16),
                      state.astype(BF16))
            u = uw[:, :HEAD_DIM] - ws[:CHUNK]
            ub = u.astype(BF16)
            o = ws[CHUNK:] + _dot(qk.astype(BF16), ub)
            k_dec = (k * jnp.exp(glast - gcol)).astype(BF16)
            s_state[h] = state * jnp.exp(glast) + _dot_tn(k_dec, ub)
            gate = gg_ref[rows, lanes]
            mix_s[rows, lanes] = (_rms(o) * gnw_ref[...] * _silu(gate)).astype(BF16)

        cos_c = cos_s[rows, :]
        sin_c = sin_s[rows, :]
        for h in range(N_HEADS):
            lanes = slice(h * HEAD_DIM, (h + 1) * HEAD_DIM)
            q = rqkv_ref[rows, lanes]
            k = rqkv_ref[rows, slice(GROUP_W + h * HEAD_DIM, GROUP_W + (h + 1) * HEAD_DIM)]
            v = rqkv_ref[rows, slice(2 * GROUP_W + h * HEAD_DIM, 2 * GROUP_W + (h + 1) * HEAD_DIM)]
            q = q * cos_c + pltpu.roll(q, HEAD_DIM // 2, axis=1) * sin_c
            k = (k * cos_c + pltpu.roll(k, HEAD_DIM // 2, axis=1) * sin_c) * HEAD_DIM ** -0.5
            vb = v.astype(BF16)
            scores = _dot_nt(q.astype(BF16), k.astype(BF16)) * dmat_ref[h]
            state = r_state[h]
            o = _dot(scores.astype(BF16), vb) + _dot((q * xi_ref[h]).astype(BF16),
                                                     state.astype(BF16))
            r_state[h] = state * chunk_gammas[h] + _dot_tn((k * zeta_ref[h]).astype(BF16), vb)
            gate = rg_ref[rows, lanes]
            out_lanes = slice(GROUP_W + h * HEAD_DIM, GROUP_W + (h + 1) * HEAD_DIM)
            mix_s[rows, out_lanes] = (_silu(gate) * _rms(o)).astype(BF16)
        return carry

    lax.fori_loop(0, N_CHUNKS, chunk_body, 0)

    h_ref[...] = x_ref[...] + _dot(mix_s[...], wout_ref[...])


def _mixer(x, gqkv, gg, rqkv, rg, ab, conv_w, alog, dtb, gnw, consts, w_out):
    seq = x.shape[0]
    n_tiles = seq // TILE
    row = lambda i: (i, 0)
    fixed2 = lambda i: (0, 0)
    fixed3 = lambda i: (0, 0, 0)
    halo_blocks = TILE // GDN_HALO
    halo_map = lambda i: (jnp.maximum(i * halo_blocks - 1, 0), 0)
    ltri, dmat, xi, zeta, cosr, sinr, cosb, sinb, sgn, chunk_gammas = consts
    in_specs = [
        pl.BlockSpec((TILE, D_MODEL), row),
        pl.BlockSpec((TILE, 3 * GROUP_W), row),
        pl.BlockSpec((GDN_HALO, 3 * GROUP_W), halo_map),
        pl.BlockSpec((TILE, GROUP_W), row),
        pl.BlockSpec((TILE, 3 * GROUP_W), row),
        pl.BlockSpec((TILE, GROUP_W), row),
        pl.BlockSpec((TILE, AB_PAD), row),
        pl.BlockSpec(conv_w.shape, fixed2),
        pl.BlockSpec(alog.shape, fixed2),
        pl.BlockSpec(dtb.shape, fixed2),
        pl.BlockSpec(gnw.shape, fixed2),
        pl.BlockSpec(ltri.shape, fixed2),
        pl.BlockSpec(dmat.shape, fixed3),
        pl.BlockSpec(xi.shape, fixed3),
        pl.BlockSpec(zeta.shape, fixed3),
        pl.BlockSpec(cosr.shape, fixed2),
        pl.BlockSpec(sinr.shape, fixed2),
        pl.BlockSpec((1, 1, HEAD_DIM), lambda i: (i, 0, 0)),
        pl.BlockSpec((1, 1, HEAD_DIM), lambda i: (i, 0, 0)),
        pl.BlockSpec(sgn.shape, fixed2),
        pl.BlockSpec(w_out.shape, fixed2, pipeline_mode=pl.Buffered(1)),
    ]
    scratch = [
        pltpu.VMEM((TILE + GDN_HALO, 3 * GROUP_W), F32),
        pltpu.VMEM((TILE, GROUP_W), F32),
        pltpu.VMEM((TILE, GROUP_W), F32),
        pltpu.VMEM((TILE, GROUP_W), F32),
        pltpu.VMEM((TILE, AB_PAD), F32),
        pltpu.VMEM((TILE, AB_PAD), F32),
        pltpu.VMEM((TILE, HEAD_DIM), F32),
        pltpu.VMEM((TILE, HEAD_DIM), F32),
        pltpu.VMEM((TILE, 2 * GROUP_W), BF16),
        pltpu.VMEM((N_HEADS, HEAD_DIM, HEAD_DIM), F32),
        pltpu.VMEM((N_HEADS, HEAD_DIM, HEAD_DIM), F32),
    ]
    return pl.pallas_call(
        functools.partial(_mixer_kernel, chunk_gammas=chunk_gammas),
        grid=(n_tiles,),
        in_specs=in_specs,
        out_specs=pl.BlockSpec((TILE, D_MODEL), row),
        out_shape=jax.ShapeDtypeStruct((seq, D_MODEL), F32),
        scratch_shapes=scratch,
        compiler_params=pltpu.CompilerParams(
            dimension_semantics=("arbitrary",), vmem_limit_bytes=VMEM_LIMIT),
        name="mixer",
    )(x, gqkv, gqkv, gg, rqkv, rg, ab, conv_w, alog, dtb, gnw, ltri, dmat, xi, zeta,
      cosr, sinr, cosb, sinb, sgn, w_out)


def _mixer_constants(seq):
    idx = np.arange(CHUNK, dtype=np.float64)
    ltri = (idx[:, None] >= idx[None, :]).astype(np.float32)
    log_gamma = np.log(1.0 - np.exp2(-5.0 - np.arange(N_HEADS, dtype=np.float64)))
    rel = idx[:, None] - idx[None, :]
    dmat = np.where(rel[None] >= 0, np.exp(np.maximum(rel, 0.0)[None] * log_gamma[:, None, None]), 0.0)
    ones = np.ones((1, 1, HEAD_DIM))
    xi = np.exp((idx[None, :] + 1.0) * log_gamma[:, None])[..., None] * ones
    zeta = np.exp((CHUNK - 1.0 - idx[None, :]) * log_gamma[:, None])[..., None] * ones
    chunk_gammas = tuple(float(g) for g in np.exp(CHUNK * log_gamma))
    angle = 1.0 / (ROPE_BASE ** np.linspace(0.0, 1.0, HEAD_DIM // 2))
    angle = np.concatenate([angle, angle])
    local = np.arange(TILE, dtype=np.float64)[:, None] * angle[None, :]
    base = (np.arange(seq // TILE, dtype=np.float64) * TILE)[:, None] * angle[None, :]
    sgn = np.concatenate([-np.ones(HEAD_DIM // 2), np.ones(HEAD_DIM // 2)])[None, :]
    f = lambda a: jnp.asarray(a, dtype=F32)
    return (f(ltri), f(dmat), f(xi), f(zeta), f(np.cos(local)), f(np.sin(local)),
            f(np.cos(base)[:, None, :]), f(np.sin(base)[:, None, :]), f(sgn), chunk_gammas)


def _mlp_kernel(h_ref, halo_ref, nw_ref, wup_ref, convw_ref, wdown_ref, fnw_ref, out_ref,
                n_s, g_s, u_s, acc_s):
    step = pl.program_id(0)
    halo = halo_ref[...]
    halo = jnp.where(step == 0, jnp.zeros_like(halo), halo)
    nw = nw_ref[...]
    n_s[0:MLP_HALO, :] = (_rms(halo) * nw).astype(BF16)
    n_s[MLP_HALO:, :] = (_rms(h_ref[...]) * nw).astype(BF16)
    acc_s[...] = h_ref[...]
    base = MLP_HALO - (FFN_CONV - 1)
    for f in range(D_FF // FF_BLOCK):
        gl = slice(f * FF_BLOCK, (f + 1) * FF_BLOCK)
        ul = slice(D_FF + f * FF_BLOCK, D_FF + (f + 1) * FF_BLOCK)
        g_s[...] = _dot(n_s[...], wup_ref[:, gl])
        u_s[...] = _dot(n_s[...], wup_ref[:, ul])
        g = convw_ref[0:1, gl] * g_s[base:base + TILE, :]
        u = convw_ref[0:1, ul] * u_s[base:base + TILE, :]
        for i in range(1, FFN_CONV):
            g = g + convw_ref[i:i + 1, gl] * g_s[base + i:base + i + TILE, :]
            u = u + convw_ref[i:i + 1, ul] * u_s[base + i:base + i + TILE, :]
        act = (_silu(g) * u).astype(BF16)
        acc_s[...] += _dot(act, wdown_ref[gl, :])
    out_ref[...] = _rms(acc_s[...]) * fnw_ref[...]


def _mlp(h, norm_w, w_up, conv_w, w_down, final_w):
    seq = h.shape[0]
    row = lambda i: (i, 0)
    fixed = lambda i: (0, 0)
    halo_blocks = TILE // MLP_HALO
    halo_map = lambda i: (jnp.maximum(i * halo_blocks - 1, 0), 0)
    return pl.pallas_call(
        _mlp_kernel,
        grid=(seq // TILE,),
        in_specs=[
            pl.BlockSpec((TILE, D_MODEL), row),
            pl.BlockSpec((MLP_HALO, D_MODEL), halo_map),
            pl.BlockSpec((1, D_MODEL), fixed),
            pl.BlockSpec(w_up.shape, fixed, pipeline_mode=pl.Buffered(1)),
            pl.BlockSpec(conv_w.shape, fixed),
            pl.BlockSpec(w_down.shape, fixed, pipeline_mode=pl.Buffered(1)),
            pl.BlockSpec((1, D_MODEL), fixed),
        ],
        out_specs=pl.BlockSpec((TILE, D_MODEL), row),
        out_shape=jax.ShapeDtypeStruct((seq, D_MODEL), F32),
        scratch_shapes=[
            pltpu.VMEM((TILE + MLP_HALO, D_MODEL), BF16),
            pltpu.VMEM((TILE + MLP_HALO, FF_BLOCK), F32),
            pltpu.VMEM((TILE + MLP_HALO, FF_BLOCK), F32),
            pltpu.VMEM((TILE, D_MODEL), F32),
        ],
        compiler_params=pltpu.CompilerParams(
            dimension_semantics=("arbitrary",), vmem_limit_bytes=VMEM_LIMIT),
        name="mlp",
    )(h, h, norm_w, w_up, conv_w, w_down, final_w)


def _pair_permutation():
    within = np.concatenate([np.arange(0, HEAD_DIM, 2), np.arange(1, HEAD_DIM, 2)])
    return np.concatenate([h * HEAD_DIM + within for h in range(N_HEADS)])


def kernel(x, attn_norm_w, w_in, gdn_conv_w, gdn_a_log, gdn_dt_bias, gdn_norm_w, w_out,
           mlp_norm_w, w_up, mlp_conv_w, w_down, final_norm_w):
    bsz, seq, _ = x.shape
    assert bsz == 1 and seq % TILE == 0 and w_in.shape[0] == 1
    x2 = x[0]
    w = w_in[0]
    g_end = 4 * GROUP_W
    r_start = g_end + 2 * N_HEADS
    perm = _pair_permutation()
    w_ret = w[:, r_start:]
    w_ab = jnp.pad(w[:, g_end:r_start], ((0, 0), (0, AB_PAD - 2 * N_HEADS)))
    w_all = jnp.concatenate([
        w[:, :g_end],
        w_ret[:, :GROUP_W][:, perm], w_ret[:, GROUP_W:2 * GROUP_W][:, perm], w_ret[:, 2 * GROUP_W:],
        w_ab], axis=1).astype(BF16)
    gqkv, gg, rqkv, rg, ab = _inproj(x2, attn_norm_w, w_all)

    pad = AB_PAD - N_HEADS
    alog = jnp.pad(gdn_a_log, ((0, 0), (0, pad)))
    dtb = jnp.pad(gdn_dt_bias, ((0, 0), (0, pad)))
    h = _mixer(x2, gqkv, gg, rqkv, rg, ab, gdn_conv_w[0], alog, dtb, gdn_norm_w,
               _mixer_constants(seq), w_out[0].astype(BF16))

    out = _mlp(h, mlp_norm_w, w_up[0].astype(BF16), mlp_conv_w[0], w_down[0].astype(BF16),
               final_norm_w[None, :])
    return out[None]
```

```python
import functools
import math

import numpy as np
import jax
import jax.numpy as jnp
from jax import lax
from jax.experimental import pallas as pl
from jax.experimental.pallas import tpu as pltpu

D_MODEL = 1024
HEAD_DIM = 128
N_HEADS = 4
GROUP_W = N_HEADS * HEAD_DIM
SHORT_CONV = 4
FFN_CONV = 3
D_FF = 2816
ROPE_BASE = 10000.0
EPS = 1e-6

TILE = 512
CHUNK = 128
N_CHUNKS = TILE // CHUNK
GDN_HALO = 8
MLP_HALO = 16
FF_BLOCK = 256
AB_PAD = 128
VMEM_LIMIT = 56 * 1024 * 1024

F32 = jnp.float32
BF16 = jnp.bfloat16


def _dot(a, b):
    return jnp.dot(a, b, preferred_element_type=F32)


def _dot_nt(a, b):
    return lax.dot_general(a, b, (((1,), (1,)), ((), ())), preferred_element_type=F32)


def _dot_tn(a, b):
    return lax.dot_general(a, b, (((0,), (0,)), ((), ())), preferred_element_type=F32)


def _silu(x):
    return x * jax.nn.sigmoid(x)


def _rms(x):
    return x * lax.rsqrt(jnp.mean(x * x, axis=-1, keepdims=True) + EPS)


def _inproj_kernel(x_ref, nw_ref, w_ref, gqkv_ref, gg_ref, rqkv_ref, rg_ref, ab_ref):
    n = (_rms(x_ref[...]) * nw_ref[...]).astype(BF16)
    col = 0
    for ref in (gqkv_ref, gg_ref, rqkv_ref, rg_ref, ab_ref):
        width = ref.shape[1]
        ref[...] = _dot(n, w_ref[:, col:col + width])
        col += width


def _inproj(x, norm_w, w_all):
    seq = x.shape[0]
    widths = (3 * GROUP_W, GROUP_W, 3 * GROUP_W, GROUP_W, AB_PAD)
    row = lambda i: (i, 0)
    fixed = lambda i: (0, 0)
    return pl.pallas_call(
        _inproj_kernel,
        grid=(seq // TILE,),
        in_specs=[
            pl.BlockSpec((TILE, D_MODEL), row),
            pl.BlockSpec((1, D_MODEL), fixed),
            pl.BlockSpec(w_all.shape, fixed, pipeline_mode=pl.Buffered(1)),
        ],
        out_specs=[pl.BlockSpec((TILE, w), row) for w in widths],
        out_shape=[jax.ShapeDtypeStruct((seq, w), F32) for w in widths],
        compiler_params=pltpu.CompilerParams(
            dimension_semantics=("arbitrary",), vmem_limit_bytes=VMEM_LIMIT),
        name="inproj",
    )(x, norm_w, w_all)


def _inv_unit_lower(ms, diag_mask, eye):
    a = [jnp.where(diag_mask, -m, 0.0) for m in ms]
    m_off = [jnp.where(diag_mask, 0.0, m).astype(BF16) for m in ms]
    p = [eye + x for x in a]
    ab = [x.astype(BF16) for x in a]
    for _ in range(4):
        ab = [_dot(x, x).astype(BF16) for x in ab]
        p = [y + _dot(y.astype(BF16), x) for x, y in zip(ab, p)]
    d_inv = [y.astype(BF16) for y in p]
    n = [_dot(d, mo) for d, mo in zip(d_inv, m_off)]
    nb = [x.astype(BF16) for x in n]
    n2 = [_dot(x, x) for x in nb]
    outer = [(eye - x) + y - _dot(xb, y.astype(BF16)) for x, xb, y in zip(n, nb, n2)]
    return [_dot(o.astype(BF16), d).astype(BF16) for o, d in zip(outer, d_inv)]


def _mixer_kernel(x_ref, gqkv_ref, halo_ref, gg_ref, rqkv_ref, rg_ref, ab_ref,
                  convw_ref, alog_ref, dtb_ref, gnw_ref, ltri_ref,
                  dmat_ref, xi_ref, zeta_ref, cosr_ref, sinr_ref, cosb_ref, sinb_ref,
                  sgn_ref, wout_ref,
                  h_ref,
                  xbuf, q_s, k_s, v_s, gc_s, beta_s, cos_s, sin_s, mix_s, s_state, r_state,
                  ubase_s, wq_s, lhs2_s, cd_s, rin_s, rkv_s, rqx_s,
                  *, chunk_gammas):
    step = pl.program_id(0)

    @pl.when(step == 0)
    def _():
        s_state[...] = jnp.zeros_like(s_state)
        r_state[...] = jnp.zeros_like(r_state)

    halo = halo_ref[...]
    xbuf[0:GDN_HALO, :] = jnp.where(step == 0, jnp.zeros_like(halo), halo)
    xbuf[GDN_HALO:, :] = gqkv_ref[...]
    for c in range(N_CHUNKS):
        r0 = c * CHUNK
        for g in range(3 * N_HEADS):
            lanes = slice(g * HEAD_DIM, (g + 1) * HEAD_DIM)
            base = r0 + GDN_HALO - (SHORT_CONV - 1)
            acc = convw_ref[0:1, lanes] * xbuf[base:base + CHUNK, lanes]
            for i in range(1, SHORT_CONV):
                acc = acc + convw_ref[i:i + 1, lanes] * xbuf[base + i:base + i + CHUNK, lanes]
            y = _silu(acc)
            out_lanes = slice((g % N_HEADS) * HEAD_DIM, (g % N_HEADS + 1) * HEAD_DIM)
            if g < N_HEADS:
                ss = jnp.sum(y * y, axis=-1, keepdims=True)
                q_s[r0:r0 + CHUNK, out_lanes] = y * (lax.rsqrt(ss + EPS) * HEAD_DIM ** -0.5)
            elif g < 2 * N_HEADS:
                ss = jnp.sum(y * y, axis=-1, keepdims=True)
                k_s[r0:r0 + CHUNK, out_lanes] = y * lax.rsqrt(ss + EPS)
            else:
                v_s[r0:r0 + CHUNK, out_lanes] = y

        ab = ab_ref[r0:r0 + CHUNK, :]
        z = ab + dtb_ref[...]
        softplus = jnp.maximum(z, 0.0) + jnp.log1p(jnp.exp(-jnp.abs(z)))
        g_step = -jnp.exp(alog_ref[...]) * softplus
        gc_s[r0:r0 + CHUNK, :] = jnp.dot(ltri_ref[...], g_step, preferred_element_type=F32,
                                         precision=lax.Precision.HIGHEST)
        beta_s[r0:r0 + CHUNK, :] = jax.nn.sigmoid(ab)

    cb = cosb_ref[0]
    sb = sinb_ref[0]
    cos_t = cosr_ref[...] * cb - sinr_ref[...] * sb
    sin_t = sinr_ref[...] * cb + cosr_ref[...] * sb
    cos_s[...] = cos_t
    sin_s[...] = sin_t * sgn_ref[...]

    row_i = lax.broadcasted_iota(jnp.int32, (CHUNK, CHUNK), 0)
    col_j = lax.broadcasted_iota(jnp.int32, (CHUNK, CHUNK), 1)
    causal = row_i >= col_j
    strict = row_i > col_j
    diag_mask = (row_i // 32) == (col_j // 32)
    eye = jnp.where(row_i == col_j, 1.0, 0.0).astype(F32)

    chains = [(c, h) for c in range(N_CHUNKS) for h in range(N_HEADS)]
    head_lanes = [slice(h * HEAD_DIM, (h + 1) * HEAD_DIM) for h in range(N_HEADS)]
    chunk_rows = [slice(c * CHUNK, (c + 1) * CHUNK) for c in range(N_CHUNKS)]

    gc_all = [gc_s[rows, :] for rows in chunk_rows]
    gc_t = [g.T for g in gc_all]
    beta_all = [beta_s[rows, :] for rows in chunk_rows]
    ms, rhs = [], []
    for idx, (c, h) in enumerate(chains):
        rows, lanes = chunk_rows[c], head_lanes[h]
        q = q_s[rows, lanes]
        k = k_s[rows, lanes]
        v = v_s[rows, lanes]
        gcol = jnp.broadcast_to(gc_all[c][:, h:h + 1], (CHUNK, HEAD_DIM))
        grow = jnp.broadcast_to(gc_t[c][h:h + 1, :], (CHUNK, CHUNK))
        beta = jnp.broadcast_to(beta_all[c][:, N_HEADS + h:N_HEADS + h + 1], (CHUNK, HEAD_DIM))
        glast = gcol[CHUNK - 1:CHUNK, :]
        decay = jnp.exp(jnp.where(causal, gcol - grow, -1e30))
        eg = jnp.exp(gcol)
        gram = _dot_nt(jnp.concatenate([q, k], axis=0).astype(BF16), k.astype(BF16))
        ms.append(jnp.where(strict, beta * gram[CHUNK:] * decay, 0.0))
        rhs.append(jnp.concatenate([v * beta, k * (beta * eg)], axis=1).astype(BF16))
        lhs2_s[idx, 0:CHUNK, :] = (gram[:CHUNK] * decay).astype(BF16)
        lhs2_s[idx, CHUNK:, :] = (k * jnp.exp(glast - gcol)).T.astype(BF16)
        wq_s[idx, CHUNK:, :] = (q * eg).astype(BF16)
        cd_s[idx] = jnp.broadcast_to(jnp.exp(glast), (8, HEAD_DIM))

    for idx, (c, h) in enumerate(chains):
        rows, lanes = chunk_rows[c], head_lanes[h]
        cos_c = cos_s[rows, :]
        sin_c = sin_s[rows, :]
        q = rqkv_ref[rows, lanes]
        k = rqkv_ref[rows, slice(GROUP_W + h * HEAD_DIM, GROUP_W + (h + 1) * HEAD_DIM)]
        v = rqkv_ref[rows, slice(2 * GROUP_W + h * HEAD_DIM, 2 * GROUP_W + (h + 1) * HEAD_DIM)]
        q = q * cos_c + pltpu.roll(q, HEAD_DIM // 2, axis=1) * sin_c
        k = (k * cos_c + pltpu.roll(k, HEAD_DIM // 2, axis=1) * sin_c) * HEAD_DIM ** -0.5
        vb = v.astype(BF16)
        scores = _dot_nt(q.astype(BF16), k.astype(BF16)) * dmat_ref[h]
        rin_s[idx] = _dot(scores.astype(BF16), vb)
        rkv_s[idx] = _dot_tn((k * zeta_ref[h]).astype(BF16), vb)
        rqx_s[idx] = (q * xi_ref[h]).astype(BF16)

    t_mats = _inv_unit_lower(ms, diag_mask, eye)
    for idx in range(len(chains)):
        uw = _dot(t_mats[idx], rhs[idx])
        ubase_s[idx] = uw[:, :HEAD_DIM]
        wq_s[idx, 0:CHUNK, :] = uw[:, HEAD_DIM:].astype(BF16)

    s_states = [s_state[h] for h in range(N_HEADS)]
    r_states = [r_state[h] for h in range(N_HEADS)]
    for c in range(N_CHUNKS):
        rows = chunk_rows[c]
        idxs = [c * N_HEADS + h for h in range(N_HEADS)]
        ws = [_dot(wq_s[i], s.astype(BF16)) for i, s in zip(idxs, s_states)]
        ubs = [(ubase_s[i] - w[:CHUNK]).astype(BF16) for i, w in zip(idxs, ws)]
        r2 = [_dot(lhs2_s[i], u) for i, u in zip(idxs, ubs)]
        ret = [rin_s[i] + _dot(rqx_s[i], r.astype(BF16)) for i, r in zip(idxs, r_states)]
        for h in range(N_HEADS):
            lanes = head_lanes[h]
            o = ws[h][CHUNK:] + r2[h][:CHUNK]
            s_states[h] = s_states[h] * cd_s[idxs[h], 0:1, :] + r2[h][CHUNK:]
            mix_s[rows, lanes] = (_rms(o) * gnw_ref[...] * _silu(gg_ref[rows, lanes])).astype(BF16)
            r_states[h] = r_states[h] * chunk_gammas[h] + rkv_s[idxs[h]]
            out_lanes = slice(GROUP_W + h * HEAD_DIM, GROUP_W + (h + 1) * HEAD_DIM)
            mix_s[rows, out_lanes] = (_silu(rg_ref[rows, lanes]) * _rms(ret[h])).astype(BF16)
    for h in range(N_HEADS):
        s_state[h] = s_states[h]
        r_state[h] = r_states[h]

    h_ref[...] = x_ref[...] + _dot(mix_s[...], wout_ref[...])


def _mixer(x, gqkv, gg, rqkv, rg, ab, conv_w, alog, dtb, gnw, consts, w_out):
    seq = x.shape[0]
    n_tiles = seq // TILE
    n_pairs = N_CHUNKS * N_HEADS
    row = lambda i: (i, 0)
    fixed2 = lambda i: (0, 0)
    fixed3 = lambda i: (0, 0, 0)
    halo_blocks = TILE // GDN_HALO
    halo_map = lambda i: (jnp.maximum(i * halo_blocks - 1, 0), 0)
    ltri, dmat, xi, zeta, cosr, sinr, cosb, sinb, sgn, chunk_gammas = consts
    in_specs = [
        pl.BlockSpec((TILE, D_MODEL), row),
        pl.BlockSpec((TILE, 3 * GROUP_W), row),
        pl.BlockSpec((GDN_HALO, 3 * GROUP_W), halo_map),
        pl.BlockSpec((TILE, GROUP_W), row),
        pl.BlockSpec((TILE, 3 * GROUP_W), row),
        pl.BlockSpec((TILE, GROUP_W), row),
        pl.BlockSpec((TILE, AB_PAD), row),
        pl.BlockSpec(conv_w.shape, fixed2),
        pl.BlockSpec(alog.shape, fixed2),
        pl.BlockSpec(dtb.shape, fixed2),
        pl.BlockSpec(gnw.shape, fixed2),
        pl.BlockSpec(ltri.shape, fixed2),
        pl.BlockSpec(dmat.shape, fixed3),
        pl.BlockSpec(xi.shape, fixed3),
        pl.BlockSpec(zeta.shape, fixed3),
        pl.BlockSpec(cosr.shape, fixed2),
        pl.BlockSpec(sinr.shape, fixed2),
        pl.BlockSpec((1, 1, HEAD_DIM), lambda i: (i, 0, 0)),
        pl.BlockSpec((1, 1, HEAD_DIM), lambda i: (i, 0, 0)),
        pl.BlockSpec(sgn.shape, fixed2),
        pl.BlockSpec(w_out.shape, fixed2, pipeline_mode=pl.Buffered(1)),
    ]
    scratch = [
        pltpu.VMEM((TILE + GDN_HALO, 3 * GROUP_W), F32),
        pltpu.VMEM((TILE, GROUP_W), F32),
        pltpu.VMEM((TILE, GROUP_W), F32),
        pltpu.VMEM((TILE, GROUP_W), F32),
        pltpu.VMEM((TILE, AB_PAD), F32),
        pltpu.VMEM((TILE, AB_PAD), F32),
        pltpu.VMEM((TILE, HEAD_DIM), F32),
        pltpu.VMEM((TILE, HEAD_DIM), F32),
        pltpu.VMEM((TILE, 2 * GROUP_W), BF16),
        pltpu.VMEM((N_HEADS, HEAD_DIM, HEAD_DIM), F32),
        pltpu.VMEM((N_HEADS, HEAD_DIM, HEAD_DIM), F32),
        pltpu.VMEM((n_pairs, CHUNK, HEAD_DIM), F32),
        pltpu.VMEM((n_pairs, 2 * CHUNK, HEAD_DIM), BF16),
        pltpu.VMEM((n_pairs, 2 * CHUNK, CHUNK), BF16),
        pltpu.VMEM((n_pairs, 8, HEAD_DIM), F32),
        pltpu.VMEM((n_pairs, CHUNK, HEAD_DIM), F32),
        pltpu.VMEM((n_pairs, HEAD_DIM, HEAD_DIM), F32),
        pltpu.VMEM((n_pairs, CHUNK, HEAD_DIM), BF16),
    ]
    return pl.pallas_call(
        functools.partial(_mixer_kernel, chunk_gammas=chunk_gammas),
        grid=(n_tiles,),
        in_specs=in_specs,
        out_specs=pl.BlockSpec((TILE, D_MODEL), row),
        out_shape=jax.ShapeDtypeStruct((seq, D_MODEL), F32),
        scratch_shapes=scratch,
        compiler_params=pltpu.CompilerParams(
            dimension_semantics=("arbitrary",), vmem_limit_bytes=VMEM_LIMIT),
        name="mixer",
    )(x, gqkv, gqkv, gg, rqkv, rg, ab, conv_w, alog, dtb, gnw, ltri, dmat, xi, zeta,
      cosr, sinr, cosb, sinb, sgn, w_out)


def _mixer_constants(seq):
    idx = np.arange(CHUNK, dtype=np.float64)
    ltri = (idx[:, None] >= idx[None, :]).astype(np.float32)
    log_gamma = np.log(1.0 - np.exp2(-5.0 - np.arange(N_HEADS, dtype=np.float64)))
    rel = idx[:, None] - idx[None, :]
    dmat = np.where(rel[None] >= 0, np.exp(np.maximum(rel, 0.0)[None] * log_gamma[:, None, None]), 0.0)
    ones = np.ones((1, 1, HEAD_DIM))
    xi = np.exp((idx[None, :] + 1.0) * log_gamma[:, None])[..., None] * ones
    zeta = np.exp((CHUNK - 1.0 - idx[None, :]) * log_gamma[:, None])[..., None] * ones
    chunk_gammas = tuple(float(g) for g in np.exp(CHUNK * log_gamma))
    angle = 1.0 / (ROPE_BASE ** np.linspace(0.0, 1.0, HEAD_DIM // 2))
    angle = np.concatenate([angle, angle])
    local = np.arange(TILE, dtype=np.float64)[:, None] * angle[None, :]
    base = (np.arange(seq // TILE, dtype=np.float64) * TILE)[:, None] * angle[None, :]
    sgn = np.concatenate([-np.ones(HEAD_DIM // 2), np.ones(HEAD_DIM // 2)])[None, :]
    f = lambda a: jnp.asarray(a, dtype=F32)
    return (f(ltri), f(dmat), f(xi), f(zeta), f(np.cos(local)), f(np.sin(local)),
            f(np.cos(base)[:, None, :]), f(np.sin(base)[:, None, :]), f(sgn), chunk_gammas)


def _mlp_kernel(h_ref, halo_ref, nw_ref, wup_ref, convw_ref, wdown_ref, fnw_ref, out_ref,
                n_s, g_s, u_s, acc_s):
    step = pl.program_id(0)
    halo = halo_ref[...]
    halo = jnp.where(step == 0, jnp.zeros_like(halo), halo)
    nw = nw_ref[...]
    n_s[0:MLP_HALO, :] = (_rms(halo) * nw).astype(BF16)
    n_s[MLP_HALO:, :] = (_rms(h_ref[...]) * nw).astype(BF16)
    acc_s[...] = h_ref[...]
    base = MLP_HALO - (FFN_CONV - 1)
    for f in range(D_FF // FF_BLOCK):
        gl = slice(f * FF_BLOCK, (f + 1) * FF_BLOCK)
        ul = slice(D_FF + f * FF_BLOCK, D_FF + (f + 1) * FF_BLOCK)
        g_s[...] = _dot(n_s[...], wup_ref[:, gl])
        u_s[...] = _dot(n_s[...], wup_ref[:, ul])
        g = convw_ref[0:1, gl] * g_s[base:base + TILE, :]
        u = convw_ref[0:1, ul] * u_s[base:base + TILE, :]
        for i in range(1, FFN_CONV):
            g = g + convw_ref[i:i + 1, gl] * g_s[base + i:base + i + TILE, :]
            u = u + convw_ref[i:i + 1, ul] * u_s[base + i:base + i + TILE, :]
        act = (_silu(g) * u).astype(BF16)
        acc_s[...] += _dot(act, wdown_ref[gl, :])
    out_ref[...] = _rms(acc_s[...]) * fnw_ref[...]


def _mlp(h, norm_w, w_up, conv_w, w_down, final_w):
    seq = h.shape[0]
    row = lambda i: (i, 0)
    fixed = lambda i: (0, 0)
    halo_blocks = TILE // MLP_HALO
    halo_map = lambda i: (jnp.maximum(i * halo_blocks - 1, 0), 0)
    return pl.pallas_call(
        _mlp_kernel,
        grid=(seq // TILE,),
        in_specs=[
            pl.BlockSpec((TILE, D_MODEL), row),
            pl.BlockSpec((MLP_HALO, D_MODEL), halo_map),
            pl.BlockSpec((1, D_MODEL), fixed),
            pl.BlockSpec(w_up.shape, fixed, pipeline_mode=pl.Buffered(1)),
            pl.BlockSpec(conv_w.shape, fixed),
            pl.BlockSpec(w_down.shape, fixed, pipeline_mode=pl.Buffered(1)),
            pl.BlockSpec((1, D_MODEL), fixed),
        ],
        out_specs=pl.BlockSpec((TILE, D_MODEL), row),
        out_shape=jax.ShapeDtypeStruct((seq, D_MODEL), F32),
        scratch_shapes=[
            pltpu.VMEM((TILE + MLP_HALO, D_MODEL), BF16),
            pltpu.VMEM((TILE + MLP_HALO, FF_BLOCK), F32),
            pltpu.VMEM((TILE + MLP_HALO, FF_BLOCK), F32),
            pltpu.VMEM((TILE, D_MODEL), F32),
        ],
        compiler_params=pltpu.CompilerParams(
            dimension_semantics=("arbitrary",), vmem_limit_bytes=VMEM_LIMIT),
        name="mlp",
    )(h, h, norm_w, w_up, conv_w, w_down, final_w)


def _pair_permutation():
    within = np.concatenate([np.arange(0, HEAD_DIM, 2), np.arange(1, HEAD_DIM, 2)])
    return np.concatenate([h * HEAD_DIM + within for h in range(N_HEADS)])


def kernel(x, attn_norm_w, w_in, gdn_conv_w, gdn_a_log, gdn_dt_bias, gdn_norm_w, w_out,
           mlp_norm_w, w_up, mlp_conv_w, w_down, final_norm_w):
    bsz, seq, _ = x.shape
    assert bsz == 1 and seq % TILE == 0 and w_in.shape[0] == 1
    x2 = x[0]
    w = w_in[0]
    g_end = 4 * GROUP_W
    r_start = g_end + 2 * N_HEADS
    perm = _pair_permutation()
    w_ret = w[:, r_start:]
    w_ab = jnp.pad(w[:, g_end:r_start], ((0, 0), (0, AB_PAD - 2 * N_HEADS)))
    w_all = jnp.concatenate([
        w[:, :g_end],
        w_ret[:, :GROUP_W][:, perm], w_ret[:, GROUP_W:2 * GROUP_W][:, perm], w_ret[:, 2 * GROUP_W:],
        w_ab], axis=1).astype(BF16)
    gqkv, gg, rqkv, rg, ab = _inproj(x2, attn_norm_w, w_all)

    pad = AB_PAD - N_HEADS
    alog = jnp.pad(gdn_a_log, ((0, 0), (0, pad)))
    dtb = jnp.pad(gdn_dt_bias, ((0, 0), (0, pad)))
    h = _mixer(x2, gqkv, gg, rqkv, rg, ab, gdn_conv_w[0], alog, dtb, gdn_norm_w,
               _mixer_constants(seq), w_out[0].astype(BF16))

    out = _mlp(h, mlp_norm_w, w_up[0].astype(BF16), mlp_conv_w[0], w_down[0].astype(BF16),
               final_norm_w[None, :])
    return out[None]
```

```python
import functools
import math

import numpy as np
import jax
import jax.numpy as jnp
from jax import lax
from jax.experimental import pallas as pl
from jax.experimental.pallas import tpu as pltpu

D_MODEL = 1024
HEAD_DIM = 128
N_HEADS = 4
GROUP_W = N_HEADS * HEAD_DIM
SHORT_CONV = 4
FFN_CONV = 3
D_FF = 2816
ROPE_BASE = 10000.0
EPS = 1e-6

TILE = 512
CHUNK = 128
N_CHUNKS = TILE // CHUNK
GDN_HALO = 8
MLP_HALO = 16
FF_BLOCK = 256
AB_PAD = 128
VMEM_LIMIT = 56 * 1024 * 1024

F32 = jnp.float32
BF16 = jnp.bfloat16


def _dot(a, b):
    return jnp.dot(a, b, preferred_element_type=F32)


def _dot_nt(a, b):
    return lax.dot_general(a, b, (((1,), (1,)), ((), ())), preferred_element_type=F32)


def _dot_tn(a, b):
    return lax.dot_general(a, b, (((0,), (0,)), ((), ())), preferred_element_type=F32)


def _silu(x):
    half = 0.5 * x
    return half + half * jnp.tanh(half)


def _rms(x):
    return x * lax.rsqrt(jnp.mean(x * x, axis=-1, keepdims=True) + EPS)


def _inproj_kernel(x_ref, nw_ref, w_ref, gqkv_ref, gg_ref, rqkv_ref, rg_ref, ab_ref):
    n = (_rms(x_ref[...]) * nw_ref[...]).astype(BF16)
    col = 0
    for ref in (gqkv_ref, gg_ref, rqkv_ref, rg_ref, ab_ref):
        width = ref.shape[1]
        ref[...] = _dot(n, w_ref[:, col:col + width])
        col += width


def _inproj(x, norm_w, w_all):
    seq = x.shape[0]
    widths = (3 * GROUP_W, GROUP_W, 3 * GROUP_W, GROUP_W, AB_PAD)
    row = lambda i: (i, 0)
    fixed = lambda i: (0, 0)
    return pl.pallas_call(
        _inproj_kernel,
        grid=(seq // TILE,),
        in_specs=[
            pl.BlockSpec((TILE, D_MODEL), row),
            pl.BlockSpec((1, D_MODEL), fixed),
            pl.BlockSpec(w_all.shape, fixed, pipeline_mode=pl.Buffered(1)),
        ],
        out_specs=[pl.BlockSpec((TILE, w), row) for w in widths],
        out_shape=[jax.ShapeDtypeStruct((seq, w), F32) for w in widths],
        compiler_params=pltpu.CompilerParams(
            dimension_semantics=("arbitrary",), vmem_limit_bytes=VMEM_LIMIT),
        name="inproj",
    )(x, norm_w, w_all)


def _inv_unit_lower(ms, diag_mask, eye):
    a = [jnp.where(diag_mask, -m, 0.0) for m in ms]
    m_off = [jnp.where(diag_mask, 0.0, m).astype(BF16) for m in ms]
    p = [eye + x for x in a]
    ab = [x.astype(BF16) for x in a]
    for _ in range(4):
        ab = [_dot(x, x).astype(BF16) for x in ab]
        p = [y + _dot(y.astype(BF16), x) for x, y in zip(ab, p)]
    d_inv = [y.astype(BF16) for y in p]
    n = [_dot(d, mo) for d, mo in zip(d_inv, m_off)]
    nb = [x.astype(BF16) for x in n]
    n2 = [_dot(x, x) for x in nb]
    outer = [(eye - x) + y - _dot(xb, y.astype(BF16)) for x, xb, y in zip(n, nb, n2)]
    return [_dot(o.astype(BF16), d).astype(BF16) for o, d in zip(outer, d_inv)]


def _mixer_kernel(x_ref, gqkv_ref, halo_ref, gg_ref, rqkv_ref, rg_ref, ab_ref,
                  convw_ref, alog_ref, dtb_ref, gnw_ref, ltri_ref,
                  dmat_ref, xi_ref, zeta_ref, cosr_ref, sinr_ref, cosb_ref, sinb_ref,
                  sgn_ref, wout_ref,
                  h_ref,
                  xbuf, q_s, k_s, v_s, gc_s, beta_s, cos_s, sin_s, mix_s, s_state, r_state,
                  ubase_s, wq_s, lhs2_s, cd_s, rin_s, rkv_s, rqx_s,
                  *, chunk_gammas):
    step = pl.program_id(0)

    @pl.when(step == 0)
    def _():
        s_state[...] = jnp.zeros_like(s_state)
        r_state[...] = jnp.zeros_like(r_state)

    halo = halo_ref[...]
    xbuf[0:GDN_HALO, :] = jnp.where(step == 0, jnp.zeros_like(halo), halo)
    xbuf[GDN_HALO:, :] = gqkv_ref[...]
    for c in range(N_CHUNKS):
        r0 = c * CHUNK
        for g in range(3 * N_HEADS):
            lanes = slice(g * HEAD_DIM, (g + 1) * HEAD_DIM)
            base = r0 + GDN_HALO - (SHORT_CONV - 1)
            acc = convw_ref[0:1, lanes] * xbuf[base:base + CHUNK, lanes]
            for i in range(1, SHORT_CONV):
                acc = acc + convw_ref[i:i + 1, lanes] * xbuf[base + i:base + i + CHUNK, lanes]
            y = _silu(acc)
            out_lanes = slice((g % N_HEADS) * HEAD_DIM, (g % N_HEADS + 1) * HEAD_DIM)
            if g < N_HEADS:
                ss = jnp.sum(y * y, axis=-1, keepdims=True)
                q_s[r0:r0 + CHUNK, out_lanes] = y * (lax.rsqrt(ss + EPS) * HEAD_DIM ** -0.5)
            elif g < 2 * N_HEADS:
                ss = jnp.sum(y * y, axis=-1, keepdims=True)
                k_s[r0:r0 + CHUNK, out_lanes] = y * lax.rsqrt(ss + EPS)
            else:
                v_s[r0:r0 + CHUNK, out_lanes] = y

        ab = ab_ref[r0:r0 + CHUNK, :]
        z = ab + dtb_ref[...]
        softplus = jnp.maximum(z, 0.0) + jnp.log1p(jnp.exp(-jnp.abs(z)))
        g_step = -jnp.exp(alog_ref[...]) * softplus
        gc_s[r0:r0 + CHUNK, :] = jnp.dot(ltri_ref[...], g_step, preferred_element_type=F32,
                                         precision=lax.Precision.HIGHEST)
        beta_s[r0:r0 + CHUNK, :] = jax.nn.sigmoid(ab)

    cb = cosb_ref[0]
    sb = sinb_ref[0]
    cos_t = cosr_ref[...] * cb - sinr_ref[...] * sb
    sin_t = sinr_ref[...] * cb + cosr_ref[...] * sb
    cos_s[...] = cos_t
    sin_s[...] = sin_t * sgn_ref[...]

    row_i = lax.broadcasted_iota(jnp.int32, (CHUNK, CHUNK), 0)
    col_j = lax.broadcasted_iota(jnp.int32, (CHUNK, CHUNK), 1)
    causal = row_i >= col_j
    strict = row_i > col_j
    diag_mask = (row_i // 32) == (col_j // 32)
    eye = jnp.where(row_i == col_j, 1.0, 0.0).astype(F32)

    chains = [(c, h) for c in range(N_CHUNKS) for h in range(N_HEADS)]
    head_lanes = [slice(h * HEAD_DIM, (h + 1) * HEAD_DIM) for h in range(N_HEADS)]
    chunk_rows = [slice(c * CHUNK, (c + 1) * CHUNK) for c in range(N_CHUNKS)]

    gc_all = [gc_s[rows, :] for rows in chunk_rows]
    gc_t = [g.T for g in gc_all]
    beta_all = [beta_s[rows, :] for rows in chunk_rows]
    ms, rhs = [], []
    for idx, (c, h) in enumerate(chains):
        rows, lanes = chunk_rows[c], head_lanes[h]
        q = q_s[rows, lanes]
        k = k_s[rows, lanes]
        v = v_s[rows, lanes]
        gcol = jnp.broadcast_to(gc_all[c][:, h:h + 1], (CHUNK, HEAD_DIM))
        grow = jnp.broadcast_to(gc_t[c][h:h + 1, :], (CHUNK, CHUNK))
        beta = jnp.broadcast_to(beta_all[c][:, N_HEADS + h:N_HEADS + h + 1], (CHUNK, HEAD_DIM))
        glast = gcol[CHUNK - 1:CHUNK, :]
        decay = jnp.exp(jnp.where(causal, gcol - grow, -1e30))
        eg = jnp.exp(gcol)
        gram = _dot_nt(jnp.concatenate([q, k], axis=0).astype(BF16), k.astype(BF16))
        ms.append(jnp.where(strict, beta * gram[CHUNK:] * decay, 0.0))
        rhs.append(jnp.concatenate([v * beta, k * (beta * eg)], axis=1).astype(BF16))
        lhs2_s[idx, 0:CHUNK, :] = (gram[:CHUNK] * decay).astype(BF16)
        lhs2_s[idx, CHUNK:, :] = (k * jnp.exp(glast - gcol)).T.astype(BF16)
        wq_s[idx, CHUNK:, :] = (q * eg).astype(BF16)
        cd_s[idx] = jnp.broadcast_to(jnp.exp(glast), (8, HEAD_DIM))

    for idx, (c, h) in enumerate(chains):
        rows, lanes = chunk_rows[c], head_lanes[h]
        cos_c = cos_s[rows, :]
        sin_c = sin_s[rows, :]
        q = rqkv_ref[rows, lanes]
        k = rqkv_ref[rows, slice(GROUP_W + h * HEAD_DIM, GROUP_W + (h + 1) * HEAD_DIM)]
        v = rqkv_ref[rows, slice(2 * GROUP_W + h * HEAD_DIM, 2 * GROUP_W + (h + 1) * HEAD_DIM)]
        q = q * cos_c + pltpu.roll(q, HEAD_DIM // 2, axis=1) * sin_c
        k = (k * cos_c + pltpu.roll(k, HEAD_DIM // 2, axis=1) * sin_c) * HEAD_DIM ** -0.5
        vb = v.astype(BF16)
        scores = _dot_nt(q.astype(BF16), k.astype(BF16)) * dmat_ref[h]
        rin_s[idx] = _dot(scores.astype(BF16), vb)
        rkv_s[idx] = _dot_tn((k * zeta_ref[h]).astype(BF16), vb)
        rqx_s[idx] = (q * xi_ref[h]).astype(BF16)

    t_mats = _inv_unit_lower(ms, diag_mask, eye)
    for idx in range(len(chains)):
        uw = _dot(t_mats[idx], rhs[idx])
        ubase_s[idx] = uw[:, :HEAD_DIM]
        wq_s[idx, 0:CHUNK, :] = uw[:, HEAD_DIM:].astype(BF16)

    s_states = [s_state[h] for h in range(N_HEADS)]
    r_states = [r_state[h] for h in range(N_HEADS)]
    for c in range(N_CHUNKS):
        rows = chunk_rows[c]
        idxs = [c * N_HEADS + h for h in range(N_HEADS)]
        ws = [_dot(wq_s[i], s.astype(BF16)) for i, s in zip(idxs, s_states)]
        ubs = [(ubase_s[i] - w[:CHUNK]).astype(BF16) for i, w in zip(idxs, ws)]
        r2 = [_dot(lhs2_s[i], u) for i, u in zip(idxs, ubs)]
        ret = [rin_s[i] + _dot(rqx_s[i], r.astype(BF16)) for i, r in zip(idxs, r_states)]
        for h in range(N_HEADS):
            lanes = head_lanes[h]
            o = ws[h][CHUNK:] + r2[h][:CHUNK]
            s_states[h] = s_states[h] * cd_s[idxs[h], 0:1, :] + r2[h][CHUNK:]
            mix_s[rows, lanes] = (_rms(o) * gnw_ref[...] * _silu(gg_ref[rows, lanes])).astype(BF16)
            r_states[h] = r_states[h] * chunk_gammas[h] + rkv_s[idxs[h]]
            out_lanes = slice(GROUP_W + h * HEAD_DIM, GROUP_W + (h + 1) * HEAD_DIM)
            mix_s[rows, out_lanes] = (_silu(rg_ref[rows, lanes]) * _rms(ret[h])).astype(BF16)
    for h in range(N_HEADS):
        s_state[h] = s_states[h]
        r_state[h] = r_states[h]

    h_ref[...] = x_ref[...] + _dot(mix_s[...], wout_ref[...])


def _mixer(x, gqkv, gg, rqkv, rg, ab, conv_w, alog, dtb, gnw, consts, w_out):
    seq = x.shape[0]
    n_tiles = seq // TILE
    n_pairs = N_CHUNKS * N_HEADS
    row = lambda i: (i, 0)
    fixed2 = lambda i: (0, 0)
    fixed3 = lambda i: (0, 0, 0)
    halo_blocks = TILE // GDN_HALO
    halo_map = lambda i: (jnp.maximum(i * halo_blocks - 1, 0), 0)
    ltri, dmat, xi, zeta, cosr, sinr, cosb, sinb, sgn, chunk_gammas = consts
    in_specs = [
        pl.BlockSpec((TILE, D_MODEL), row),
        pl.BlockSpec((TILE, 3 * GROUP_W), row),
        pl.BlockSpec((GDN_HALO, 3 * GROUP_W), halo_map),
        pl.BlockSpec((TILE, GROUP_W), row),
        pl.BlockSpec((TILE, 3 * GROUP_W), row),
        pl.BlockSpec((TILE, GROUP_W), row),
        pl.BlockSpec((TILE, AB_PAD), row),
        pl.BlockSpec(conv_w.shape, fixed2),
        pl.BlockSpec(alog.shape, fixed2),
        pl.BlockSpec(dtb.shape, fixed2),
        pl.BlockSpec(gnw.shape, fixed2),
        pl.BlockSpec(ltri.shape, fixed2),
        pl.BlockSpec(dmat.shape, fixed3),
        pl.BlockSpec(xi.shape, fixed3),
        pl.BlockSpec(zeta.shape, fixed3),
        pl.BlockSpec(cosr.shape, fixed2),
        pl.BlockSpec(sinr.shape, fixed2),
        pl.BlockSpec((1, 1, HEAD_DIM), lambda i: (i, 0, 0)),
        pl.BlockSpec((1, 1, HEAD_DIM), lambda i: (i, 0, 0)),
        pl.BlockSpec(sgn.shape, fixed2),
        pl.BlockSpec(w_out.shape, fixed2, pipeline_mode=pl.Buffered(1)),
    ]
    scratch = [
        pltpu.VMEM((TILE + GDN_HALO, 3 * GROUP_W), F32),
        pltpu.VMEM((TILE, GROUP_W), F32),
        pltpu.VMEM((TILE, GROUP_W), F32),
        pltpu.VMEM((TILE, GROUP_W), F32),
        pltpu.VMEM((TILE, AB_PAD), F32),
        pltpu.VMEM((TILE, AB_PAD), F32),
        pltpu.VMEM((TILE, HEAD_DIM), F32),
        pltpu.VMEM((TILE, HEAD_DIM), F32),
        pltpu.VMEM((TILE, 2 * GROUP_W), BF16),
        pltpu.VMEM((N_HEADS, HEAD_DIM, HEAD_DIM), F32),
        pltpu.VMEM((N_HEADS, HEAD_DIM, HEAD_DIM), F32),
        pltpu.VMEM((n_pairs, CHUNK, HEAD_DIM), F32),
        pltpu.VMEM((n_pairs, 2 * CHUNK, HEAD_DIM), BF16),
        pltpu.VMEM((n_pairs, 2 * CHUNK, CHUNK), BF16),
        pltpu.VMEM((n_pairs, 8, HEAD_DIM), F32),
        pltpu.VMEM((n_pairs, CHUNK, HEAD_DIM), F32),
        pltpu.VMEM((n_pairs, HEAD_DIM, HEAD_DIM), F32),
        pltpu.VMEM((n_pairs, CHUNK, HEAD_DIM), BF16),
    ]
    return pl.pallas_call(
        functools.partial(_mixer_kernel, chunk_gammas=chunk_gammas),
        grid=(n_tiles,),
        in_specs=in_specs,
        out_specs=pl.BlockSpec((TILE, D_MODEL), row),
        out_shape=jax.ShapeDtypeStruct((seq, D_MODEL), F32),
        scratch_shapes=scratch,
        compiler_params=pltpu.CompilerParams(
            dimension_semantics=("arbitrary",), vmem_limit_bytes=VMEM_LIMIT),
        name="mixer",
    )(x, gqkv, gqkv, gg, rqkv, rg, ab, conv_w, alog, dtb, gnw, ltri, dmat, xi, zeta,
      cosr, sinr, cosb, sinb, sgn, w_out)


def _mixer_constants(seq):
    idx = np.arange(CHUNK, dtype=np.float64)
    ltri = (idx[:, None] >= idx[None, :]).astype(np.float32)
    log_gamma = np.log(1.0 - np.exp2(-5.0 - np.arange(N_HEADS, dtype=np.float64)))
    rel = idx[:, None] - idx[None, :]
    dmat = np.where(rel[None] >= 0, np.exp(np.maximum(rel, 0.0)[None] * log_gamma[:, None, None]), 0.0)
    ones = np.ones((1, 1, HEAD_DIM))
    xi = np.exp((idx[None, :] + 1.0) * log_gamma[:, None])[..., None] * ones
    zeta = np.exp((CHUNK - 1.0 - idx[None, :]) * log_gamma[:, None])[..., None] * ones
    chunk_gammas = tuple(float(g) for g in np.exp(CHUNK * log_gamma))
    angle = 1.0 / (ROPE_BASE ** np.linspace(0.0, 1.0, HEAD_DIM // 2))
    angle = np.concatenate([angle, angle])
    local = np.arange(TILE, dtype=np.float64)[:, None] * angle[None, :]
    base = (np.arange(seq // TILE, dtype=np.float64) * TILE)[:, None] * angle[None, :]
    sgn = np.concatenate([-np.ones(HEAD_DIM // 2), np.ones(HEAD_DIM // 2)])[None, :]
    f = lambda a: jnp.asarray(a, dtype=F32)
    return (f(ltri), f(dmat), f(xi), f(zeta), f(np.cos(local)), f(np.sin(local)),
            f(np.cos(base)[:, None, :]), f(np.sin(base)[:, None, :]), f(sgn), chunk_gammas)


def _mlp_kernel(h_ref, halo_ref, nw_ref, wup_ref, convw_ref, wdown_ref, fnw_ref, out_ref,
                n_s, g_s, u_s, act_s):
    step = pl.program_id(0)
    halo = halo_ref[...]
    halo = jnp.where(step == 0, jnp.zeros_like(halo), halo)
    nw = nw_ref[...]
    n_s[0:MLP_HALO, :] = (_rms(halo) * nw).astype(BF16)
    n_s[MLP_HALO:, :] = (_rms(h_ref[...]) * nw).astype(BF16)
    base = MLP_HALO - (FFN_CONV - 1)
    n_blocks = D_FF // FF_BLOCK

    def up_proj(f):
        slot = f % 2
        g_s[slot] = _dot(n_s[...], wup_ref[:, f * FF_BLOCK:(f + 1) * FF_BLOCK])
        u_s[slot] = _dot(n_s[...], wup_ref[:, D_FF + f * FF_BLOCK:D_FF + (f + 1) * FF_BLOCK])

    up_proj(0)
    for f in range(n_blocks):
        if f + 1 < n_blocks:
            up_proj(f + 1)
        slot = f % 2
        gl = slice(f * FF_BLOCK, (f + 1) * FF_BLOCK)
        ul = slice(D_FF + f * FF_BLOCK, D_FF + (f + 1) * FF_BLOCK)
        g = convw_ref[0:1, gl] * g_s[slot, base:base + TILE, :]
        u = convw_ref[0:1, ul] * u_s[slot, base:base + TILE, :]
        for i in range(1, FFN_CONV):
            g = g + convw_ref[i:i + 1, gl] * g_s[slot, base + i:base + i + TILE, :]
            u = u + convw_ref[i:i + 1, ul] * u_s[slot, base + i:base + i + TILE, :]
        act_s[:, gl] = (_silu(g) * u).astype(BF16)
    out = h_ref[...] + _dot(act_s[...], wdown_ref[...])
    out_ref[...] = _rms(out) * fnw_ref[...]


def _mlp(h, norm_w, w_up, conv_w, w_down, final_w):
    seq = h.shape[0]
    row = lambda i: (i, 0)
    fixed = lambda i: (0, 0)
    halo_blocks = TILE // MLP_HALO
    halo_map = lambda i: (jnp.maximum(i * halo_blocks - 1, 0), 0)
    return pl.pallas_call(
        _mlp_kernel,
        grid=(seq // TILE,),
        in_specs=[
            pl.BlockSpec((TILE, D_MODEL), row),
            pl.BlockSpec((MLP_HALO, D_MODEL), halo_map),
            pl.BlockSpec((1, D_MODEL), fixed),
            pl.BlockSpec(w_up.shape, fixed, pipeline_mode=pl.Buffered(1)),
            pl.BlockSpec(conv_w.shape, fixed),
            pl.BlockSpec(w_down.shape, fixed, pipeline_mode=pl.Buffered(1)),
            pl.BlockSpec((1, D_MODEL), fixed),
        ],
        out_specs=pl.BlockSpec((TILE, D_MODEL), row),
        out_shape=jax.ShapeDtypeStruct((seq, D_MODEL), F32),
        scratch_shapes=[
            pltpu.VMEM((TILE + MLP_HALO, D_MODEL), BF16),
            pltpu.VMEM((2, TILE + MLP_HALO, FF_BLOCK), F32),
            pltpu.VMEM((2, TILE + MLP_HALO, FF_BLOCK), F32),
            pltpu.VMEM((TILE, D_FF), BF16),
        ],
        compiler_params=pltpu.CompilerParams(
            dimension_semantics=("arbitrary",), vmem_limit_bytes=VMEM_LIMIT),
        name="mlp",
    )(h, h, norm_w, w_up, conv_w, w_down, final_w)


def _pair_permutation():
    within = np.concatenate([np.arange(0, HEAD_DIM, 2), np.arange(1, HEAD_DIM, 2)])
    return np.concatenate([h * HEAD_DIM + within for h in range(N_HEADS)])


def kernel(x, attn_norm_w, w_in, gdn_conv_w, gdn_a_log, gdn_dt_bias, gdn_norm_w, w_out,
           mlp_norm_w, w_up, mlp_conv_w, w_down, final_norm_w):
    bsz, seq, _ = x.shape
    assert bsz == 1 and seq % TILE == 0 and w_in.shape[0] == 1
    x2 = x[0]
    w = w_in[0]
    g_end = 4 * GROUP_W
    r_start = g_end + 2 * N_HEADS
    perm = _pair_permutation()
    w_ret = w[:, r_start:]
    w_ab = jnp.pad(w[:, g_end:r_start], ((0, 0), (0, AB_PAD - 2 * N_HEADS)))
    w_all = jnp.concatenate([
        w[:, :g_end],
        w_ret[:, :GROUP_W][:, perm], w_ret[:, GROUP_W:2 * GROUP_W][:, perm], w_ret[:, 2 * GROUP_W:],
        w_ab], axis=1).astype(BF16)
    gqkv, gg, rqkv, rg, ab = _inproj(x2, attn_norm_w, w_all)

    pad = AB_PAD - N_HEADS
    alog = jnp.pad(gdn_a_log, ((0, 0), (0, pad)))
    dtb = jnp.pad(gdn_dt_bias, ((0, 0), (0, pad)))
    h = _mixer(x2, gqkv, gg, rqkv, rg, ab, gdn_conv_w[0], alog, dtb, gdn_norm_w,
               _mixer_constants(seq), w_out[0].astype(BF16))

    out = _mlp(h, mlp_norm_w, w_up[0].astype(BF16), mlp_conv_w[0], w_down[0].astype(BF16),
               final_norm_w[None, :])
    return out[None]
```
